```python
import jax, jax.numpy as jnp
from jax import lax
import numpy as np

D_MODEL = 2048
BATCH = 2
SEQ = 16384
DEPTH = 2

GRID_W = 64
NA_HEADS = 8
NA_HEAD_DIM = 128
NA_WIDTH = NA_HEADS * NA_HEAD_DIM
NA_MAX_ROWS = 8
NA_COLS = 16
SG_GROUPS = 8
SG_GROUP_DIM = 128
SG_WIDTH = SG_GROUPS * SG_GROUP_DIM
SG_CHUNK = 128
SPLIT_WIDTHS = (NA_WIDTH, NA_WIDTH, NA_WIDTH, NA_WIDTH, SG_WIDTH, SG_WIDTH, SG_WIDTH, D_MODEL, D_MODEL)
N_IN = sum(SPLIT_WIDTHS)
SPLIT_POINTS = tuple(int(s) for s in np.cumsum(SPLIT_WIDTHS)[:-1])
RMS_EPS = 1e-6
LN_EPS = 1e-5

kernel_name = "hybrid_natten_gmlp_encoder"


def rms_norm(x, g):
    xf = x.astype(jnp.float32)
    y = xf * lax.rsqrt(jnp.mean(xf * xf, axis=-1, keepdims=True) + RMS_EPS)
    return (y * g.astype(jnp.float32)).astype(x.dtype)


def layer_norm(x, g, b):
    xf = x.astype(jnp.float32)
    mu = jnp.mean(xf, axis=-1, keepdims=True)
    xc = xf - mu
    var = jnp.mean(xc * xc, axis=-1, keepdims=True)
    y = xc * lax.rsqrt(var + LN_EPS) * g.astype(jnp.float32) + b.astype(jnp.float32)
    return y.astype(x.dtype)


def neighbourhood_attention(q, k, v, rpb, rows):
    kr = min(NA_MAX_ROWS, rows)
    col = jnp.arange(GRID_W)
    col_start = jnp.clip(col - NA_COLS // 2, 0, GRID_W - NA_COLS)
    col_idx = col_start[:, None] + jnp.arange(NA_COLS)[None, :]
    dc = col_idx - col[:, None] + (NA_COLS - 1)
    rpb_c = rpb[:, :, dc]
    scale = NA_HEAD_DIM ** -0.5
    b = q.shape[0]

    def one_row(r):
        rs = jnp.clip(r - kr // 2, 0, rows - kr)
        k_rows = lax.dynamic_slice_in_dim(k, rs, kr, axis=1)
        v_rows = lax.dynamic_slice_in_dim(v, rs, kr, axis=1)
        k_win = k_rows[:, :, col_idx]
        v_win = v_rows[:, :, col_idx]
        q_r = lax.dynamic_index_in_dim(q, r, axis=1, keepdims=False)
        dr = rs + jnp.arange(kr) - r + (NA_MAX_ROWS - 1)
        bias = rpb_c[:, dr].transpose(0, 2, 1, 3)
        s = jnp.einsum('bqhd,bkqchd->bhqkc', q_r, k_win).astype(jnp.float32) * scale
        s = s + bias[None].astype(jnp.float32)
        p = jax.nn.softmax(s.reshape(b, NA_HEADS, GRID_W, kr * NA_COLS), axis=-1)
        p = p.reshape(s.shape).astype(v.dtype)
        return jnp.einsum('bhqkc,bkqchd->bqhd', p, v_win)

    out = lax.map(one_row, jnp.arange(rows))
    return out.transpose(1, 0, 2, 3, 4)


def spatial_gating(u, v, ln_g, ln_b, w_s, b_s):
    b, t, _ = v.shape
    v = layer_norm(v, ln_g, ln_b)
    vc = v.reshape(b, t // SG_CHUNK, SG_CHUNK, SG_GROUPS, SG_GROUP_DIM)
    s = jnp.einsum('gst,bntgc->bnsgc', w_s, vc) + b_s.T[None, None, :, :, None]
    return u * s.reshape(b, t, SG_WIDTH)


def hybrid_layer(x, pre_g, post_g, w_in, rpb, ln_g, ln_b, w_s, b_s, w_pa, w_pb, w_out):
    b, t, _ = x.shape
    rows = t // GRID_W
    h = rms_norm(x, pre_g)
    proj = jnp.einsum('btd,dn->btn', h, w_in)
    q, k, v, z_a, u, vs, z_b, g_a, g_b = jnp.split(proj, SPLIT_POINTS, axis=-1)
    grid = (b, rows, GRID_W, NA_HEADS, NA_HEAD_DIM)
    y_a = neighbourhood_attention(q.reshape(grid), k.reshape(grid), v.reshape(grid), rpb, rows)
    y_a = y_a.reshape(b, t, NA_WIDTH) * jax.nn.silu(z_a)
    y_b = spatial_gating(u, vs, ln_g, ln_b, w_s, b_s) * jax.nn.silu(z_b)
    merged = (jax.nn.sigmoid(g_a) * jnp.einsum('btc,cd->btd', y_a, w_pa)
              + jax.nn.sigmoid(g_b) * jnp.einsum('btc,cd->btd', y_b, w_pb))
    out = jnp.einsum('btd,de->bte', merged, w_out)
    return x + rms_norm(out, post_g)


def setup_inputs(seed: int = 0) -> dict:
    key = jax.random.key(seed)
    ks = jax.random.split(key, 14)
    f32 = jnp.float32
    n = lambda k, shape: jax.random.normal(k, shape, f32)
    return {
        "x": n(ks[0], (BATCH, SEQ, D_MODEL)),
        "pre_norm_g": 1.0 + 0.02 * n(ks[1], (DEPTH, D_MODEL)),
        "post_norm_g": 1.0 + 0.02 * n(ks[2], (DEPTH, D_MODEL)),
        "w_in": n(ks[3], (DEPTH, D_MODEL, N_IN)) * D_MODEL ** -0.5,
        "na_rpb": 0.1 * n(ks[4], (DEPTH, NA_HEADS, 2 * NA_MAX_ROWS - 1, 2 * NA_COLS - 1)),
        "sg_ln_g": 1.0 + 0.02 * n(ks[5], (DEPTH, SG_WIDTH)),
        "sg_ln_b": 0.02 * n(ks[6], (DEPTH, SG_WIDTH)),
        "sg_w": n(ks[7], (DEPTH, SG_GROUPS, SG_CHUNK, SG_CHUNK)) * SG_CHUNK ** -0.5,
        "sg_b": 1.0 + 0.02 * n(ks[8], (DEPTH, SG_GROUPS, SG_CHUNK)),
        "w_proj_a": n(ks[9], (DEPTH, NA_WIDTH, D_MODEL)) * NA_WIDTH ** -0.5,
        "w_proj_b": n(ks[10], (DEPTH, SG_WIDTH, D_MODEL)) * SG_WIDTH ** -0.5,
        "w_out": n(ks[11], (DEPTH, D_MODEL, D_MODEL)) * D_MODEL ** -0.5,
    }


def reference(x, pre_norm_g, post_norm_g, w_in, na_rpb, sg_ln_g, sg_ln_b, sg_w, sg_b,
              w_proj_a, w_proj_b, w_out):
    for l in range(DEPTH):
        x = hybrid_layer(x, pre_norm_g[l], post_norm_g[l], w_in[l], na_rpb[l], sg_ln_g[l],
                         sg_ln_b[l], sg_w[l], sg_b[l], w_proj_a[l], w_proj_b[l], w_out[l])
    return x
```

```python
import functools

import numpy as np
import jax
import jax.numpy as jnp
from jax import lax
from jax.experimental import pallas as pl
from jax.experimental.pallas import tpu as pltpu

D_MODEL = 2048
GRID_W = 64
NA_HEADS = 8
NA_HEAD_DIM = 128
NA_WIDTH = NA_HEADS * NA_HEAD_DIM
NA_MAX_ROWS = 8
NA_COLS = 16
SG_GROUPS = 8
SG_GROUP_DIM = 128
SG_WIDTH = SG_GROUPS * SG_GROUP_DIM
SG_CHUNK = 128
RMS_EPS = 1e-6
LN_EPS = 1e-5

SEG = 1024
N_IN = 11 * SEG
_REF_OFFSETS = dict(q=0, k=1024, v=2048, z_a=3072, u=4096, vs=5120, z_b=6144, g_a=7168, g_b=9216)
COL_GA, COL_GB, COL_Q, COL_K, COL_V, COL_ZA, COL_U, COL_VS, COL_ZB = 0, 2, 4, 5, 6, 7, 8, 9, 10

VMEM_LIMIT_BYTES = 56 * 1024 * 1024

BF16 = jnp.bfloat16
F32 = jnp.float32


def _column_permutation():
    order = ["g_a", "g_b", "q", "k", "v", "z_a", "u", "vs", "z_b"]
    widths = dict(g_a=2048, g_b=2048)
    cols = []
    for name in order:
        w = widths.get(name, 1024)
        cols.append(np.arange(_REF_OFFSETS[name], _REF_OFFSETS[name] + w))
    return np.concatenate(cols)


IN_TM = 1024
IN_TN = 1024
IN_NORM_ROWS = 64


def _in_proj_kernel(x_ref, g_ref, w_ref, o_ref, h_ref):
    @pl.when(pl.program_id(1) == 0)
    def _():
        g = g_ref[...]

        def body(c, carry):
            rows = pl.ds(pl.multiple_of(c * IN_NORM_ROWS, IN_NORM_ROWS), IN_NORM_ROWS)
            x = x_ref[rows, :]
            ms = jnp.mean(x * x, axis=-1, keepdims=True)
            h_ref[rows, :] = (x * lax.rsqrt(ms + RMS_EPS) * g).astype(BF16)
            return carry

        lax.fori_loop(0, IN_TM // IN_NORM_ROWS, body, 0)

    o_ref[...] = jnp.dot(h_ref[...], w_ref[...], preferred_element_type=F32).astype(o_ref.dtype)


def _in_proj(x2, g, w):
    m = x2.shape[0]
    return pl.pallas_call(
        _in_proj_kernel,
        grid=(m // IN_TM, N_IN // IN_TN),
        in_specs=[
            pl.BlockSpec((IN_TM, D_MODEL), lambda i, j: (i, 0)),
            pl.BlockSpec((1, D_MODEL), lambda i, j: (0, 0)),
            pl.BlockSpec((D_MODEL, IN_TN), lambda i, j: (0, j)),
        ],
        out_specs=pl.BlockSpec((IN_TM, IN_TN), lambda i, j: (i, j)),
        out_shape=jax.ShapeDtypeStruct((m, N_IN), BF16),
        scratch_shapes=[pltpu.VMEM((IN_TM, D_MODEL), BF16)],
        compiler_params=pltpu.CompilerParams(
            dimension_semantics=("parallel", "arbitrary"),
            vmem_limit_bytes=VMEM_LIMIT_BYTES,
        ),
        name="in_proj",
    )(x2, g, w)


NA_ROWS_PER_STEP = 16
NA_TQ = NA_ROWS_PER_STEP * GRID_W
NA_KEYS = NA_MAX_ROWS * GRID_W


def _natten_bias_table(rpb):
    col = np.arange(GRID_W)
    col_start = np.clip(col - NA_COLS // 2, 0, GRID_W - NA_COLS)
    kc = np.arange(GRID_W)
    valid = (kc[None, :] >= col_start[:, None]) & (kc[None, :] < col_start[:, None] + NA_COLS)
    dc = np.clip(kc[None, :] - col[:, None] + (NA_COLS - 1), 0, 2 * NA_COLS - 2)
    dr = np.arange(NA_MAX_ROWS)[:, None] + np.arange(NA_MAX_ROWS)[None, :]
    t = rpb[:, dr][:, :, :, dc]
    t = jnp.where(valid[None, None, None], t, -jnp.inf)
    t = t.transpose(0, 1, 3, 2, 4)
    return t.reshape(NA_HEADS, NA_MAX_ROWS, GRID_W, NA_KEYS).astype(F32)


def _natten_kernel(q_ref, k_ref, v_ref, z_ref, bias_ref, o_ref, *, rows):
    i = pl.program_id(2)
    scale = NA_HEAD_DIM ** -0.5

    def body(t, carry):
        r = i * NA_ROWS_PER_STEP + t
        rs = jnp.clip(r - NA_MAX_ROWS // 2, 0, rows - NA_MAX_ROWS)
        variant = rs - r + (NA_MAX_ROWS - 1)
        qrows = pl.ds(pl.multiple_of(t * GRID_W, GRID_W), GRID_W)
        krows = pl.ds(pl.multiple_of(rs * GRID_W, GRID_W), NA_KEYS)
        q = q_ref[qrows, :]
        k = k_ref[krows, :]
        v = v_ref[krows, :]
        s = lax.dot_general(q, k, (((1,), (1,)), ((), ())), preferred_element_type=F32)
        s = s * scale + bias_ref[variant]
        m = jnp.max(s, axis=-1, keepdims=True)
        p = jnp.exp(s - m)
        l = jnp.sum(p, axis=-1, keepdims=True)
        pv = jnp.dot(p.astype(BF16), v, preferred_element_type=F32)
        z = z_ref[qrows, :].astype(F32)
        o_ref[qrows, :] = ((pv / l) * (z * jax.nn.sigmoid(z))).astype(o_ref.dtype)
        return carry

    lax.fori_loop(0, NA_ROWS_PER_STEP, body, 0)


def _natten(proj3, bias):
    b, t, _ = proj3.shape
    rows = t // GRID_W
    tok = lambda col: pl.BlockSpec((None, NA_TQ, NA_HEAD_DIM), lambda bi, h, i: (bi, i, col * NA_HEADS + h))
    seq = lambda col: pl.BlockSpec((None, t, NA_HEAD_DIM), lambda bi, h, i: (bi, 0, col * NA_HEADS + h))
    return pl.pallas_call(
        functools.partial(_natten_kernel, rows=rows),
        grid=(b, NA_HEADS, t // NA_TQ),
        in_specs=[
            tok(COL_Q), seq(COL_K), seq(COL_V), tok(COL_ZA),
            pl.BlockSpec((None, NA_MAX_ROWS, GRID_W, NA_KEYS), lambda bi, h, i: (h, 0, 0, 0)),
        ],
        out_specs=pl.BlockSpec((None, NA_TQ, NA_HEAD_DIM), lambda bi, h, i: (bi, i, h)),
        out_shape=jax.ShapeDtypeStruct((b, t, NA_WIDTH), BF16),
        compiler_params=pltpu.CompilerParams(
            dimension_semantics=("parallel", "parallel", "parallel"),
            vmem_limit_bytes=VMEM_LIMIT_BYTES,
        ),
        name="natten",
    )(proj3, proj3, proj3, proj3, bias)


SG_TM = 512


def _sgu_kernel(u_ref, v_ref, z_ref, lng_ref, lnb_ref, w_ref, bs_ref, o_ref, vn_ref):
    v = v_ref[...].astype(F32)
    mu = jnp.mean(v, axis=-1, keepdims=True)
    xc = v - mu
    var = jnp.mean(xc * xc, axis=-1, keepdims=True)
    vn_ref[...] = (xc * lax.rsqrt(var + LN_EPS) * lng_ref[...] + lnb_ref[...]).astype(BF16)

    for c in range(SG_TM // SG_CHUNK):
        rows = slice(c * SG_CHUNK, (c + 1) * SG_CHUNK)
        for g in range(SG_GROUPS):
            cols = slice(g * SG_GROUP_DIM, (g + 1) * SG_GROUP_DIM)
            s = jnp.dot(w_ref[g], vn_ref[rows, cols], preferred_element_type=F32) + bs_ref[g]
            z = z_ref[rows, cols].astype(F32)
            u = u_ref[rows, cols].astype(F32)
            o_ref[rows, cols] = (u * s * (z * jax.nn.sigmoid(z))).astype(o_ref.dtype)


def _sgu(proj, ln_g, ln_b, w_s, b_s):
    m = proj.shape[0]
    seg = lambda col: pl.BlockSpec((SG_TM, SG_WIDTH), lambda i: (i, col))
    const = lambda shape: pl.BlockSpec(shape, lambda i: (0,) * len(shape))
    return pl.pallas_call(
        _sgu_kernel,
        grid=(m // SG_TM,),
        in_specs=[
            seg(COL_U), seg(COL_VS), seg(COL_ZB),
            const((1, SG_WIDTH)), const((1, SG_WIDTH)),
            const((SG_GROUPS, SG_CHUNK, SG_CHUNK)), const((SG_GROUPS, SG_CHUNK, 1)),
        ],
        out_specs=pl.BlockSpec((SG_TM, SG_WIDTH), lambda i: (i, 0)),
        out_shape=jax.ShapeDtypeStruct((m, SG_WIDTH), BF16),
        scratch_shapes=[pltpu.VMEM((SG_TM, SG_WIDTH), BF16)],
        compiler_params=pltpu.CompilerParams(
            dimension_semantics=("parallel",),
            vmem_limit_bytes=VMEM_LIMIT_BYTES,
        ),
        name="sgu",
    )(proj, proj, proj, ln_g, ln_b, w_s, b_s)


OUT_TM = 256


def _out_proj_kernel(ya_ref, yb_ref, ga_ref, gb_ref, x_ref, wpa_ref, wpb_ref, wout_ref, pg_ref, o_ref):
    a = jnp.dot(ya_ref[...], wpa_ref[...], preferred_element_type=F32)
    b = jnp.dot(yb_ref[...], wpb_ref[...], preferred_element_type=F32)
    ga = jax.nn.sigmoid(ga_ref[...].astype(F32))
    gb = jax.nn.sigmoid(gb_ref[...].astype(F32))
    merged = (ga * a + gb * b).astype(BF16)
    out = jnp.dot(merged, wout_ref[...], preferred_element_type=F32)
    ms = jnp.mean(out * out, axis=-1, keepdims=True)
    o_ref[...] = x_ref[...] + out * lax.rsqrt(ms + RMS_EPS) * pg_ref[...]


def _out_proj(y_a, y_b, proj, x2, w_pa, w_pb, w_out, post_g):
    m = x2.shape[0]
    const = lambda shape: pl.BlockSpec(shape, lambda i: (0,) * len(shape), pipeline_mode=pl.Buffered(1))
    return pl.pallas_call(
        _out_proj_kernel,
        grid=(m // OUT_TM,),
        in_specs=[
            pl.BlockSpec((OUT_TM, NA_WIDTH), lambda i: (i, 0)),
            pl.BlockSpec((OUT_TM, SG_WIDTH), lambda i: (i, 0)),
            pl.BlockSpec((OUT_TM, D_MODEL), lambda i: (i, COL_GA // 2)),
            pl.BlockSpec((OUT_TM, D_MODEL), lambda i: (i, COL_GB // 2)),
            pl.BlockSpec((OUT_TM, D_MODEL), lambda i: (i, 0)),
            const((NA_WIDTH, D_MODEL)), const((SG_WIDTH, D_MODEL)), const((D_MODEL, D_MODEL)),
            const((1, D_MODEL)),
        ],
        out_specs=pl.BlockSpec((OUT_TM, D_MODEL), lambda i: (i, 0)),
        out_shape=jax.ShapeDtypeStruct((m, D_MODEL), F32),
        compiler_params=pltpu.CompilerParams(
            dimension_semantics=("parallel",),
            vmem_limit_bytes=VMEM_LIMIT_BYTES,
        ),
        name="out_proj",
    )(y_a, y_b, proj, proj, x2, w_pa, w_pb, w_out, post_g)


def _layer(x, pre_g, post_g, w_in, rpb, ln_g, ln_b, w_s, b_s, w_pa, w_pb, w_out):
    b, t, d = x.shape
    m = b * t
    x2 = x.reshape(m, d)
    w_in_p = w_in[:, _column_permutation()].astype(BF16)
    proj = _in_proj(x2, pre_g.reshape(1, d), w_in_p)
    y_a = _natten(proj.reshape(b, t, N_IN), _natten_bias_table(rpb)).reshape(m, NA_WIDTH)
    y_b = _sgu(proj, ln_g.reshape(1, SG_WIDTH), ln_b.reshape(1, SG_WIDTH),
               w_s.astype(BF16), b_s.reshape(SG_GROUPS, SG_CHUNK, 1))
    out = _out_proj(y_a, y_b, proj, x2, w_pa.astype(BF16), w_pb.astype(BF16), w_out.astype(BF16),
                    post_g.reshape(1, d))
    return out.reshape(b, t, d)


def kernel(x, pre_norm_g, post_norm_g, w_in, na_rpb, sg_ln_g, sg_ln_b, sg_w, sg_b, w_proj_a, w_proj_b, w_out):
    for l in range(w_in.shape[0]):
        x = _layer(x, pre_norm_g[l], post_norm_g[l], w_in[l], na_rpb[l], sg_ln_g[l], sg_ln_b[l],
                   sg_w[l], sg_b[l], w_proj_a[l], w_proj_b[l], w_out[l])
    return x
```

```python
import functools

import numpy as np
import jax
import jax.numpy as jnp
from jax import lax
from jax.experimental import pallas as pl
from jax.experimental.pallas import tpu as pltpu

D_MODEL = 2048
GRID_W = 64
NA_HEADS = 8
NA_HEAD_DIM = 128
NA_WIDTH = NA_HEADS * NA_HEAD_DIM
NA_MAX_ROWS = 8
NA_COLS = 16
SG_GROUPS = 8
SG_GROUP_DIM = 128
SG_WIDTH = SG_GROUPS * SG_GROUP_DIM
SG_CHUNK = 128
RMS_EPS = 1e-6
LN_EPS = 1e-5

SEG = 1024
N_IN = 11 * SEG
_REF_OFFSETS = dict(q=0, k=1024, v=2048, z_a=3072, u=4096, vs=5120, z_b=6144, g_a=7168, g_b=9216)
COL_GA, COL_GB, COL_Q, COL_K, COL_V, COL_ZA, COL_U, COL_VS, COL_ZB = 0, 2, 4, 5, 6, 7, 8, 9, 10

VMEM_LIMIT_BYTES = 56 * 1024 * 1024

BF16 = jnp.bfloat16
F32 = jnp.float32


def _reorder_in_proj_columns(w_in):
    split = _REF_OFFSETS["g_a"]
    return jnp.concatenate([w_in[:, split:], w_in[:, :split]], axis=1)


IN_TM = 1024
IN_TN = 1024
IN_NORM_ROWS = 64


def _in_proj_kernel(x_ref, g_ref, w_ref, o_ref, h_ref):
    @pl.when(pl.program_id(1) == 0)
    def _():
        g = g_ref[...]

        def body(c, carry):
            rows = pl.ds(pl.multiple_of(c * IN_NORM_ROWS, IN_NORM_ROWS), IN_NORM_ROWS)
            x = x_ref[rows, :]
            ms = jnp.mean(x * x, axis=-1, keepdims=True)
            h_ref[rows, :] = (x * lax.rsqrt(ms + RMS_EPS) * g).astype(BF16)
            return carry

        lax.fori_loop(0, IN_TM // IN_NORM_ROWS, body, 0)

    o_ref[...] = jnp.dot(h_ref[...], w_ref[...], preferred_element_type=F32).astype(o_ref.dtype)


def _in_proj(x2, g, w):
    m = x2.shape[0]
    return pl.pallas_call(
        _in_proj_kernel,
        grid=(m // IN_TM, N_IN // IN_TN),
        in_specs=[
            pl.BlockSpec((IN_TM, D_MODEL), lambda i, j: (i, 0)),
            pl.BlockSpec((1, D_MODEL), lambda i, j: (0, 0)),
            pl.BlockSpec((D_MODEL, IN_TN), lambda i, j: (0, j)),
        ],
        out_specs=pl.BlockSpec((IN_TM, IN_TN), lambda i, j: (i, j)),
        out_shape=jax.ShapeDtypeStruct((m, N_IN), BF16),
        scratch_shapes=[pltpu.VMEM((IN_TM, D_MODEL), BF16)],
        compiler_params=pltpu.CompilerParams(
            dimension_semantics=("parallel", "arbitrary"),
            vmem_limit_bytes=VMEM_LIMIT_BYTES,
        ),
        name="in_proj",
    )(x2, g, w)


NA_ROWS_PER_STEP = 16
NA_UNROLL = 16
NA_TQ = NA_ROWS_PER_STEP * GRID_W
NA_KEYS = NA_MAX_ROWS * GRID_W


def _natten_bias_table(rpb):
    col = np.arange(GRID_W)
    col_start = np.clip(col - NA_COLS // 2, 0, GRID_W - NA_COLS)
    valid = (col[None, :] >= col_start[:, None]) & (col[None, :] < col_start[:, None] + NA_COLS)
    pad = GRID_W - NA_COLS
    rp = jnp.pad(rpb.astype(F32), ((0, 0), (0, 0), (pad, pad)))
    first = NA_COLS - 1 + pad
    t = jnp.stack([rp[:, :, first - c:first - c + GRID_W] for c in range(GRID_W)], axis=1)
    t = jnp.where(valid[None, :, None, :], t, -jnp.inf)
    t = jnp.stack([t[:, :, o:o + NA_MAX_ROWS] for o in range(NA_MAX_ROWS)], axis=1)
    return t.reshape(NA_HEADS, NA_MAX_ROWS, GRID_W, NA_KEYS)


def _natten_kernel(q_ref, k_ref, v_ref, z_ref, bias_ref, o_ref, *, rows):
    i = pl.program_id(2)
    scale = NA_HEAD_DIM ** -0.5

    def body(g, carry):
        qrows, krows, scores = [], [], []
        for u in range(NA_UNROLL):
            t = g * NA_UNROLL + u
            r = i * NA_ROWS_PER_STEP + t
            rs = jnp.clip(r - NA_MAX_ROWS // 2, 0, rows - NA_MAX_ROWS)
            variant = rs - r + (NA_MAX_ROWS - 1)
            qrows.append(pl.ds(pl.multiple_of(t * GRID_W, GRID_W), GRID_W))
            krows.append(pl.ds(pl.multiple_of(rs * GRID_W, GRID_W), NA_KEYS))
            s = lax.dot_general(q_ref[qrows[u], :], k_ref[krows[u], :], (((1,), (1,)), ((), ())),
                                preferred_element_type=F32)
            scores.append(s * scale + bias_ref[variant])
        probs, sums = [], []
        for s in scores:
            m = jnp.max(s, axis=-1, keepdims=True)
            p = jnp.exp(s - m)
            sums.append(jnp.sum(p, axis=-1, keepdims=True))
            probs.append(p.astype(BF16))
        for u in range(NA_UNROLL):
            pv = jnp.dot(probs[u], v_ref[krows[u], :], preferred_element_type=F32)
            z = z_ref[qrows[u], :].astype(F32)
            o_ref[qrows[u], :] = ((pv / sums[u]) * (z * jax.nn.sigmoid(z))).astype(o_ref.dtype)
        return carry

    lax.fori_loop(0, NA_ROWS_PER_STEP // NA_UNROLL, body, 0)


def _natten(proj3, bias):
    b, t, _ = proj3.shape
    rows = t // GRID_W
    tok = lambda col: pl.BlockSpec((None, NA_TQ, NA_HEAD_DIM), lambda bi, h, i: (bi, i, col * NA_HEADS + h))
    seq = lambda col: pl.BlockSpec((None, t, NA_HEAD_DIM), lambda bi, h, i: (bi, 0, col * NA_HEADS + h))
    return pl.pallas_call(
        functools.partial(_natten_kernel, rows=rows),
        grid=(b, NA_HEADS, t // NA_TQ),
        in_specs=[
            tok(COL_Q), seq(COL_K), seq(COL_V), tok(COL_ZA),
            pl.BlockSpec((None, NA_MAX_ROWS, GRID_W, NA_KEYS), lambda bi, h, i: (h, 0, 0, 0)),
        ],
        out_specs=pl.BlockSpec((None, NA_TQ, NA_HEAD_DIM), lambda bi, h, i: (bi, i, h)),
        out_shape=jax.ShapeDtypeStruct((b, t, NA_WIDTH), BF16),
        compiler_params=pltpu.CompilerParams(
            dimension_semantics=("parallel", "parallel", "parallel"),
            vmem_limit_bytes=VMEM_LIMIT_BYTES,
        ),
        name="natten",
    )(proj3, proj3, proj3, proj3, bias)


SG_TM = 512


def _sgu_kernel(u_ref, v_ref, z_ref, lng_ref, lnb_ref, w_ref, bs_ref, o_ref, vn_ref):
    v = v_ref[...].astype(F32)
    mu = jnp.mean(v, axis=-1, keepdims=True)
    xc = v - mu
    var = jnp.mean(xc * xc, axis=-1, keepdims=True)
    vn_ref[...] = (xc * lax.rsqrt(var + LN_EPS) * lng_ref[...] + lnb_ref[...]).astype(BF16)

    for c in range(SG_TM // SG_CHUNK):
        rows = slice(c * SG_CHUNK, (c + 1) * SG_CHUNK)
        for g in range(SG_GROUPS):
            cols = slice(g * SG_GROUP_DIM, (g + 1) * SG_GROUP_DIM)
            s = jnp.dot(w_ref[g], vn_ref[rows, cols], preferred_element_type=F32) + bs_ref[g]
            z = z_ref[rows, cols].astype(F32)
            u = u_ref[rows, cols].astype(F32)
            o_ref[rows, cols] = (u * s * (z * jax.nn.sigmoid(z))).astype(o_ref.dtype)


def _sgu(proj, ln_g, ln_b, w_s, b_s):
    m = proj.shape[0]
    seg = lambda col: pl.BlockSpec((SG_TM, SG_WIDTH), lambda i: (i, col))
    const = lambda shape: pl.BlockSpec(shape, lambda i: (0,) * len(shape))
    return pl.pallas_call(
        _sgu_kernel,
        grid=(m // SG_TM,),
        in_specs=[
            seg(COL_U), seg(COL_VS), seg(COL_ZB),
            const((1, SG_WIDTH)), const((1, SG_WIDTH)),
            const((SG_GROUPS, SG_CHUNK, SG_CHUNK)), const((SG_GROUPS, SG_CHUNK, 1)),
        ],
        out_specs=pl.BlockSpec((SG_TM, SG_WIDTH), lambda i: (i, 0)),
        out_shape=jax.ShapeDtypeStruct((m, SG_WIDTH), BF16),
        scratch_shapes=[pltpu.VMEM((SG_TM, SG_WIDTH), BF16)],
        compiler_params=pltpu.CompilerParams(
            dimension_semantics=("parallel",),
            vmem_limit_bytes=VMEM_LIMIT_BYTES,
        ),
        name="sgu",
    )(proj, proj, proj, ln_g, ln_b, w_s, b_s)


OUT_TM = 256


def _out_proj_kernel(ya_ref, yb_ref, ga_ref, gb_ref, x_ref, wpa_ref, wpb_ref, wout_ref, pg_ref, o_ref):
    a = jnp.dot(ya_ref[...], wpa_ref[...], preferred_element_type=F32)
    b = jnp.dot(yb_ref[...], wpb_ref[...], preferred_element_type=F32)
    ga = jax.nn.sigmoid(ga_ref[...].astype(F32))
    gb = jax.nn.sigmoid(gb_ref[...].astype(F32))
    merged = (ga * a + gb * b).astype(BF16)
    out = jnp.dot(merged, wout_ref[...], preferred_element_type=F32)
    ms = jnp.mean(out * out, axis=-1, keepdims=True)
    o_ref[...] = x_ref[...] + out * lax.rsqrt(ms + RMS_EPS) * pg_ref[...]


def _out_proj(y_a, y_b, proj, x2, w_pa, w_pb, w_out, post_g):
    m = x2.shape[0]
    const = lambda shape: pl.BlockSpec(shape, lambda i: (0,) * len(shape), pipeline_mode=pl.Buffered(1))
    return pl.pallas_call(
        _out_proj_kernel,
        grid=(m // OUT_TM,),
        in_specs=[
            pl.BlockSpec((OUT_TM, NA_WIDTH), lambda i: (i, 0)),
            pl.BlockSpec((OUT_TM, SG_WIDTH), lambda i: (i, 0)),
            pl.BlockSpec((OUT_TM, D_MODEL), lambda i: (i, COL_GA // 2)),
            pl.BlockSpec((OUT_TM, D_MODEL), lambda i: (i, COL_GB // 2)),
            pl.BlockSpec((OUT_TM, D_MODEL), lambda i: (i, 0)),
            const((NA_WIDTH, D_MODEL)), const((SG_WIDTH, D_MODEL)), const((D_MODEL, D_MODEL)),
            const((1, D_MODEL)),
        ],
        out_specs=pl.BlockSpec((OUT_TM, D_MODEL), lambda i: (i, 0)),
        out_shape=jax.ShapeDtypeStruct((m, D_MODEL), F32),
        compiler_params=pltpu.CompilerParams(
            dimension_semantics=("parallel",),
            vmem_limit_bytes=VMEM_LIMIT_BYTES,
        ),
        name="out_proj",
    )(y_a, y_b, proj, proj, x2, w_pa, w_pb, w_out, post_g)


def _layer(x, pre_g, post_g, w_in, rpb, ln_g, ln_b, w_s, b_s, w_pa, w_pb, w_out):
    b, t, d = x.shape
    m = b * t
    x2 = x.reshape(m, d)
    w_in_p = _reorder_in_proj_columns(w_in).astype(BF16)
    proj = _in_proj(x2, pre_g.reshape(1, d), w_in_p)
    y_a = _natten(proj.reshape(b, t, N_IN), _natten_bias_table(rpb)).reshape(m, NA_WIDTH)
    y_b = _sgu(proj, ln_g.reshape(1, SG_WIDTH), ln_b.reshape(1, SG_WIDTH),
               w_s.astype(BF16), b_s.reshape(SG_GROUPS, SG_CHUNK, 1))
    out = _out_proj(y_a, y_b, proj, x2, w_pa.astype(BF16), w_pb.astype(BF16), w_out.astype(BF16),
                    post_g.reshape(1, d))
    return out.reshape(b, t, d)


def kernel(x, pre_norm_g, post_norm_g, w_in, na_rpb, sg_ln_g, sg_ln_b, sg_w, sg_b, w_proj_a, w_proj_b, w_out):
    for l in range(w_in.shape[0]):
        x = _layer(x, pre_norm_g[l], post_norm_g[l], w_in[l], na_rpb[l], sg_ln_g[l], sg_ln_b[l],
                   sg_w[l], sg_b[l], w_proj_a[l], w_proj_b[l], w_out[l])
    return x
```

```python
import functools

import numpy as np
import jax
import jax.numpy as jnp
from jax import lax
from jax.experimental import pallas as pl
from jax.experimental.pallas import tpu as pltpu

D_MODEL = 2048
GRID_W = 64
NA_HEADS = 8
NA_HEAD_DIM = 128
NA_WIDTH = NA_HEADS * NA_HEAD_DIM
NA_MAX_ROWS = 8
NA_COLS = 16
SG_GROUPS = 8
SG_GROUP_DIM = 128
SG_WIDTH = SG_GROUPS * SG_GROUP_DIM
SG_CHUNK = 128
RMS_EPS = 1e-6
LN_EPS = 1e-5

LANES = 128
SEG = 1024
SEG_BLOCKS = SEG // LANES
N_IN = 11 * SEG
N_BLOCKS = N_IN // LANES
COL_GA, COL_GB, COL_Q, COL_K, COL_V, COL_ZA, COL_U, COL_VS, COL_ZB = 0, 2, 4, 5, 6, 7, 8, 9, 10
REF_SEGMENT_SHIFT = 7
N_SEGMENTS = N_IN // SEG

VMEM_LIMIT_BYTES = 56 * 1024 * 1024

BF16 = jnp.bfloat16
F32 = jnp.float32


def _silu(z):
    return z * jax.nn.sigmoid(z)


IN_TM = 1024
IN_TN = SEG
IN_NORM_ROWS = 64


def _in_proj_kernel(x_ref, g_ref, w_ref, o_ref, h_ref):
    @pl.when(pl.program_id(1) == 0)
    def _():
        g = g_ref[...]

        def body(c, carry):
            rows = pl.ds(pl.multiple_of(c * IN_NORM_ROWS, IN_NORM_ROWS), IN_NORM_ROWS)
            x = x_ref[rows, :]
            ms = jnp.mean(x * x, axis=-1, keepdims=True)
            h_ref[rows, :] = (x * lax.rsqrt(ms + RMS_EPS) * g).astype(BF16)
            return carry

        lax.fori_loop(0, IN_TM // IN_NORM_ROWS, body, 0)

    res = jnp.dot(h_ref[...], w_ref[...], preferred_element_type=F32)
    for c in range(SEG_BLOCKS):
        o_ref[c] = res[:, c * LANES:(c + 1) * LANES].astype(o_ref.dtype)


def _in_proj(x2, g, w, layer):
    m = x2.shape[0]
    return pl.pallas_call(
        _in_proj_kernel,
        grid=(m // IN_TM, N_SEGMENTS),
        in_specs=[
            pl.BlockSpec((IN_TM, D_MODEL), lambda i, j: (i, 0)),
            pl.BlockSpec((None, 1, D_MODEL), lambda i, j: (layer, 0, 0)),
            pl.BlockSpec((None, D_MODEL, IN_TN), lambda i, j: (layer, 0, (j + REF_SEGMENT_SHIFT) % N_SEGMENTS)),
        ],
        out_specs=pl.BlockSpec((SEG_BLOCKS, IN_TM, LANES), lambda i, j: (j, i, 0)),
        out_shape=jax.ShapeDtypeStruct((N_BLOCKS, m, LANES), BF16),
        scratch_shapes=[pltpu.VMEM((IN_TM, D_MODEL), BF16)],
        compiler_params=pltpu.CompilerParams(
            dimension_semantics=("parallel", "arbitrary"),
            vmem_limit_bytes=VMEM_LIMIT_BYTES,
        ),
        name="in_proj",
    )(x2, g, w)


NA_ROWS_PER_STEP = 16
NA_TQ = NA_ROWS_PER_STEP * GRID_W
NA_KEYS = NA_MAX_ROWS * GRID_W


def _natten_bias_table(rpb):
    col = np.arange(GRID_W)
    col_start = np.clip(col - NA_COLS // 2, 0, GRID_W - NA_COLS)
    valid = (col[None, :] >= col_start[:, None]) & (col[None, :] < col_start[:, None] + NA_COLS)
    lead = rpb.shape[:-1]
    pad = GRID_W - NA_COLS
    period = 2 * GRID_W
    rp = jnp.pad(rpb.astype(F32), ((0, 0),) * len(lead) + ((pad, period - (2 * NA_COLS - 1) - pad),))
    y = jnp.tile(rp, GRID_W)[..., :GRID_W * (period - 1)].reshape(*lead, GRID_W, period - 1)
    t = y[..., GRID_W - 1:2 * GRID_W - 1]
    t = jnp.where(valid, t, -jnp.inf)
    t = jnp.moveaxis(t, -3, -2)
    t = jnp.stack([t[..., o:o + NA_MAX_ROWS, :] for o in range(NA_MAX_ROWS)], axis=-4)
    return t.reshape(*lead[:-1], NA_MAX_ROWS, GRID_W, NA_KEYS)


def _natten_kernel(q_ref, k_ref, v_ref, z_ref, bias_ref, o_ref, *, rows):
    i = pl.program_id(2)
    scale = NA_HEAD_DIM ** -0.5
    qrows, krows, scores = [], [], []
    for t in range(NA_ROWS_PER_STEP):
        r = i * NA_ROWS_PER_STEP + t
        rs = jnp.clip(r - NA_MAX_ROWS // 2, 0, rows - NA_MAX_ROWS)
        variant = rs - r + (NA_MAX_ROWS - 1)
        qrows.append(slice(t * GRID_W, (t + 1) * GRID_W))
        krows.append(pl.ds(pl.multiple_of(rs * GRID_W, GRID_W), NA_KEYS))
        s = lax.dot_general(q_ref[qrows[t], :], k_ref[krows[t], :], (((1,), (1,)), ((), ())),
                            preferred_element_type=F32)
        scores.append(s * scale + bias_ref[variant])
    probs, sums = [], []
    for s in scores:
        m = jnp.max(s, axis=-1, keepdims=True)
        p = jnp.exp(s - m)
        sums.append(jnp.sum(p, axis=-1, keepdims=True))
        probs.append(p.astype(BF16))
    for t in range(NA_ROWS_PER_STEP):
        pv = jnp.dot(probs[t], v_ref[krows[t], :], preferred_element_type=F32)
        z = z_ref[qrows[t], :].astype(F32)
        o_ref[qrows[t], :] = ((pv / sums[t]) * _silu(z)).astype(o_ref.dtype)


def _natten(proj, bias, layer, batch):
    m = proj.shape[1]
    t = m // batch
    rows = t // GRID_W
    steps = t // NA_TQ
    tok = lambda col: pl.BlockSpec((None, NA_TQ, LANES), lambda b, h, i: (col * SEG_BLOCKS + h, b * steps + i, 0))
    seq = lambda col: pl.BlockSpec((None, t, LANES), lambda b, h, i: (col * SEG_BLOCKS + h, b, 0))
    return pl.pallas_call(
        functools.partial(_natten_kernel, rows=rows),
        grid=(batch, NA_HEADS, steps),
        in_specs=[
            tok(COL_Q), seq(COL_K), seq(COL_V), tok(COL_ZA),
            pl.BlockSpec((None, None, NA_MAX_ROWS, GRID_W, NA_KEYS), lambda b, h, i: (layer, h, 0, 0, 0)),
        ],
        out_specs=pl.BlockSpec((NA_TQ, NA_HEAD_DIM), lambda b, h, i: (b * steps + i, h)),
        out_shape=jax.ShapeDtypeStruct((m, NA_WIDTH), BF16),
        compiler_params=pltpu.CompilerParams(
            dimension_semantics=("parallel", "parallel", "parallel"),
            vmem_limit_bytes=VMEM_LIMIT_BYTES,
        ),
        name="natten",
    )(proj, proj, proj, proj, bias)


SG_TM = 512


def _sgu_kernel(u_ref, v_ref, z_ref, lng_ref, lnb_ref, w_ref, bs_ref, o_ref, vn_ref):
    v = [v_ref[g].astype(F32) for g in range(SG_GROUPS)]
    mu = jnp.sum(functools.reduce(jnp.add, v), axis=-1, keepdims=True) * (1.0 / SG_WIDTH)
    xc = [vg - mu for vg in v]
    var = jnp.sum(functools.reduce(jnp.add, [x * x for x in xc]), axis=-1, keepdims=True) * (1.0 / SG_WIDTH)
    inv = lax.rsqrt(var + LN_EPS)
    for g in range(SG_GROUPS):
        vn_ref[g] = (xc[g] * inv * lng_ref[g] + lnb_ref[g]).astype(BF16)

    for c in range(SG_TM // SG_CHUNK):
        rows = slice(c * SG_CHUNK, (c + 1) * SG_CHUNK)
        for g in range(SG_GROUPS):
            s = jnp.dot(w_ref[g], vn_ref[g, rows, :], preferred_element_type=F32) + bs_ref[g]
            y = u_ref[g, rows, :].astype(F32) * s * _silu(z_ref[g, rows, :].astype(F32))
            o_ref[rows, g * SG_GROUP_DIM:(g + 1) * SG_GROUP_DIM] = y.astype(o_ref.dtype)


def _sgu(proj, ln_g, ln_b, w_s, b_s, layer):
    m = proj.shape[1]
    seg = lambda col: pl.BlockSpec((SEG_BLOCKS, SG_TM, LANES), lambda i: (col, i, 0))
    param = lambda shape: pl.BlockSpec((None,) + shape, lambda i: (layer,) + (0,) * len(shape))
    return pl.pallas_call(
        _sgu_kernel,
        grid=(m // SG_TM,),
        in_specs=[
            seg(COL_U), seg(COL_VS), seg(COL_ZB),
            param((SG_GROUPS, 1, SG_GROUP_DIM)), param((SG_GROUPS, 1, SG_GROUP_DIM)),
            param((SG_GROUPS, SG_CHUNK, SG_CHUNK)), param((SG_GROUPS, SG_CHUNK, 1)),
        ],
        out_specs=pl.BlockSpec((SG_TM, SG_WIDTH), lambda i: (i, 0)),
        out_shape=jax.ShapeDtypeStruct((m, SG_WIDTH), BF16),
        scratch_shapes=[pltpu.VMEM((SG_GROUPS, SG_TM, SG_GROUP_DIM), BF16)],
        compiler_params=pltpu.CompilerParams(
            dimension_semantics=("parallel",),
            vmem_limit_bytes=VMEM_LIMIT_BYTES,
        ),
        name="sgu",
    )(proj, proj, proj, ln_g, ln_b, w_s, b_s)


OUT_TM = 256
GATE_BLOCKS = D_MODEL // LANES


def _out_proj_kernel(ya_ref, yb_ref, ga_ref, gb_ref, x_ref, wpa_ref, wpb_ref, wout_ref, pg_ref, o_ref, m_ref):
    a = jnp.dot(ya_ref[...], wpa_ref[...], preferred_element_type=F32)
    b = jnp.dot(yb_ref[...], wpb_ref[...], preferred_element_type=F32)
    for c in range(GATE_BLOCKS):
        cols = slice(c * LANES, (c + 1) * LANES)
        ga = jax.nn.sigmoid(ga_ref[c].astype(F32))
        gb = jax.nn.sigmoid(gb_ref[c].astype(F32))
        m_ref[:, cols] = (ga * a[:, cols] + gb * b[:, cols]).astype(BF16)
    out = jnp.dot(m_ref[...], wout_ref[...], preferred_element_type=F32)
    ms = jnp.mean(out * out, axis=-1, keepdims=True)
    o_ref[...] = x_ref[...] + out * lax.rsqrt(ms + RMS_EPS) * pg_ref[...]


def _out_proj(y_a, y_b, proj, x2, w_pa, w_pb, w_out, post_g, layer):
    m = x2.shape[0]
    param = lambda shape: pl.BlockSpec((None,) + shape, lambda i: (layer,) + (0,) * len(shape),
                                       pipeline_mode=pl.Buffered(1))
    gate = lambda col: pl.BlockSpec((GATE_BLOCKS, OUT_TM, LANES), lambda i: (col // 2, i, 0))
    return pl.pallas_call(
        _out_proj_kernel,
        grid=(m // OUT_TM,),
        in_specs=[
            pl.BlockSpec((OUT_TM, NA_WIDTH), lambda i: (i, 0)),
            pl.BlockSpec((OUT_TM, SG_WIDTH), lambda i: (i, 0)),
            gate(COL_GA), gate(COL_GB),
            pl.BlockSpec((OUT_TM, D_MODEL), lambda i: (i, 0)),
            param((NA_WIDTH, D_MODEL)), param((SG_WIDTH, D_MODEL)), param((D_MODEL, D_MODEL)),
            param((1, D_MODEL)),
        ],
        out_specs=pl.BlockSpec((OUT_TM, D_MODEL), lambda i: (i, 0)),
        out_shape=jax.ShapeDtypeStruct((m, D_MODEL), F32),
        scratch_shapes=[pltpu.VMEM((OUT_TM, D_MODEL), BF16)],
        compiler_params=pltpu.CompilerParams(
            dimension_semantics=("parallel",),
            vmem_limit_bytes=VMEM_LIMIT_BYTES,
        ),
        name="out_proj",
    )(y_a, y_b, proj, proj, x2, w_pa, w_pb, w_out, post_g)


def kernel(x, pre_norm_g, post_norm_g, w_in, na_rpb, sg_ln_g, sg_ln_b, sg_w, sg_b, w_proj_a, w_proj_b, w_out):
    b, t, d = x.shape
    depth = w_in.shape[0]
    x2 = x.reshape(b * t, d)
    pre_g = pre_norm_g.reshape(depth, 1, d)
    post_g = post_norm_g.reshape(depth, 1, d)
    w_in_b, w_pa_b, w_pb_b, w_out_b, w_s_b = (w.astype(BF16) for w in (w_in, w_proj_a, w_proj_b, w_out, sg_w))
    bias = _natten_bias_table(na_rpb)
    ln_g = sg_ln_g.reshape(depth, SG_GROUPS, 1, SG_GROUP_DIM)
    ln_b = sg_ln_b.reshape(depth, SG_GROUPS, 1, SG_GROUP_DIM)
    b_s = sg_b.reshape(depth, SG_GROUPS, SG_CHUNK, 1)
    for l in range(depth):
        proj = _in_proj(x2, pre_g, w_in_b, l)
        y_a = _natten(proj, bias, l, b)
        y_b = _sgu(proj, ln_g, ln_b, w_s_b, b_s, l)
        x2 = _out_proj(y_a, y_b, proj, x2, w_pa_b, w_pb_b, w_out_b, post_g, l)
    return x2.reshape(b, t, d)
```

```python
import functools

import numpy as np
import jax
import jax.numpy as jnp
from jax import lax
from jax.experimental import pallas as pl
from jax.experimental.pallas import tpu as pltpu

D_MODEL = 2048
GRID_W = 64
NA_HEADS = 8
NA_HEAD_DIM = 128
NA_WIDTH = NA_HEADS * NA_HEAD_DIM
NA_MAX_ROWS = 8
NA_COLS = 16
SG_GROUPS = 8
SG_GROUP_DIM = 128
SG_WIDTH = SG_GROUPS * SG_GROUP_DIM
SG_CHUNK = 128
RMS_EPS = 1e-6
LN_EPS = 1e-5

LANES = 128
SEG = 1024
SEG_BLOCKS = SEG // LANES
N_IN = 11 * SEG
N_BLOCKS = N_IN // LANES
COL_GA, COL_GB, COL_Q, COL_K, COL_V, COL_ZA, COL_U, COL_VS, COL_ZB = 0, 2, 4, 5, 6, 7, 8, 9, 10
REF_SEGMENT_SHIFT = 7
N_SEGMENTS = N_IN // SEG

VMEM_LIMIT_BYTES = 56 * 1024 * 1024

BF16 = jnp.bfloat16
F32 = jnp.float32


def _silu(z):
    return z * jax.nn.sigmoid(z)


def _rms_norm_rows(x, g):
    ms = jnp.mean(x * x, axis=-1, keepdims=True)
    return x * lax.rsqrt(ms + RMS_EPS) * g


NORM_TM = 1024


def _rms_norm_kernel(x_ref, g_ref, o_ref):
    o_ref[...] = _rms_norm_rows(x_ref[...], g_ref[...]).astype(o_ref.dtype)


def _rms_norm(x2, g, layer):
    m = x2.shape[0]
    return pl.pallas_call(
        _rms_norm_kernel,
        grid=(m // NORM_TM,),
        in_specs=[
            pl.BlockSpec((NORM_TM, D_MODEL), lambda i: (i, 0)),
            pl.BlockSpec((None, 1, D_MODEL), lambda i: (layer, 0, 0)),
        ],
        out_specs=pl.BlockSpec((NORM_TM, D_MODEL), lambda i: (i, 0)),
        out_shape=jax.ShapeDtypeStruct((m, D_MODEL), BF16),
        compiler_params=pltpu.CompilerParams(dimension_semantics=("parallel",), vmem_limit_bytes=VMEM_LIMIT_BYTES),
        name="rms_norm",
    )(x2, g)


IN_TM = 2048
IN_TN = SEG


def _in_proj_kernel(h_ref, w_ref, o_ref):
    res = jnp.dot(h_ref[...], w_ref[...].astype(BF16), preferred_element_type=F32)
    for c in range(SEG_BLOCKS):
        o_ref[c] = res[:, c * LANES:(c + 1) * LANES].astype(o_ref.dtype)


def _in_proj(h, w, layer):
    m = h.shape[0]
    return pl.pallas_call(
        _in_proj_kernel,
        grid=(m // IN_TM, N_SEGMENTS),
        in_specs=[
            pl.BlockSpec((IN_TM, D_MODEL), lambda i, j: (i, 0)),
            pl.BlockSpec((None, D_MODEL, IN_TN), lambda i, j: (layer, 0, (j + REF_SEGMENT_SHIFT) % N_SEGMENTS)),
        ],
        out_specs=pl.BlockSpec((SEG_BLOCKS, IN_TM, LANES), lambda i, j: (j, i, 0)),
        out_shape=jax.ShapeDtypeStruct((N_BLOCKS, m, LANES), BF16),
        compiler_params=pltpu.CompilerParams(
            dimension_semantics=("parallel", "parallel"),
            vmem_limit_bytes=VMEM_LIMIT_BYTES,
        ),
        name="in_proj",
    )(h, w)


NA_ROWS_PER_STEP = 128
NA_LAG = 10
NA_TQ = NA_ROWS_PER_STEP * GRID_W
NA_KEYS = NA_MAX_ROWS * GRID_W
NA_BIAS_ROWS = 2 * NA_MAX_ROWS - 2


def _natten_bias_table(rpb):
    col = np.arange(GRID_W)
    col_start = np.clip(col - NA_COLS // 2, 0, GRID_W - NA_COLS)
    valid = (col[None, :] >= col_start[:, None]) & (col[None, :] < col_start[:, None] + NA_COLS)
    rp = rpb.astype(F32) * (NA_HEAD_DIM ** 0.5)
    dc = col[None, :] - col[:, None] + (NA_COLS - 1)
    select = (dc[:, :, None] == np.arange(2 * NA_COLS - 1)) & valid[:, :, None]
    t = jnp.einsum("...j,ckj->...ck", rp, jnp.asarray(select, F32), precision=lax.Precision.HIGHEST)
    t = jnp.where(valid, t, -jnp.inf)
    return jnp.concatenate([t[..., :-1, :, :], t[..., 1:, :, :]], axis=-1)


def _natten_kernel(q_ref, k_ref, v_ref, z_ref, bias_ref, o_ref, *, rows):
    i = pl.program_id(2)
    exp2_scale = (NA_HEAD_DIM ** -0.5) * np.log2(np.e)

    def key_rows(t):
        r = i * NA_ROWS_PER_STEP + t
        rs = jnp.clip(r - NA_MAX_ROWS // 2, 0, rows - NA_MAX_ROWS)
        return rs - r + (NA_MAX_ROWS - 1), pl.ds(pl.multiple_of(rs * GRID_W, GRID_W), NA_KEYS)

    def scores(t):
        variant, krows = key_rows(t)
        s = lax.dot_general(q_ref[t * GRID_W:(t + 1) * GRID_W, :], k_ref[krows, :], (((1,), (1,)), ((), ())),
                            preferred_element_type=F32)
        bias = [bias_ref[variant + 2 * j] for j in range(NA_MAX_ROWS // 2)]
        return s + jnp.concatenate(bias, axis=1)

    def finish(t, s):
        _, krows = key_rows(t)
        qrows = slice(t * GRID_W, (t + 1) * GRID_W)
        m = jnp.max(s, axis=-1, keepdims=True)
        p = jnp.exp2((s - m) * exp2_scale)
        l = jnp.sum(p, axis=-1, keepdims=True)
        pv = jnp.dot(p.astype(BF16), v_ref[krows, :], preferred_element_type=F32)
        z = z_ref[qrows, :].astype(F32)
        o_ref[qrows, :] = ((pv / l) * _silu(z)).astype(o_ref.dtype)

    pending = {}
    for step in range(NA_ROWS_PER_STEP + NA_LAG):
        if step < NA_ROWS_PER_STEP:
            pending[step] = scores(step)
        if step >= NA_LAG:
            finish(step - NA_LAG, pending.pop(step - NA_LAG))


def _natten(proj, bias, layer, batch):
    m = proj.shape[1]
    t = m // batch
    rows = t // GRID_W
    steps = t // NA_TQ
    tok = lambda col: pl.BlockSpec((None, NA_TQ, LANES), lambda b, h, i: (col * SEG_BLOCKS + h, b * steps + i, 0))
    seq = lambda col: pl.BlockSpec((None, t, LANES), lambda b, h, i: (col * SEG_BLOCKS + h, b, 0))
    return pl.pallas_call(
        functools.partial(_natten_kernel, rows=rows),
        grid=(batch, NA_HEADS, steps),
        in_specs=[
            tok(COL_Q), seq(COL_K), seq(COL_V), tok(COL_ZA),
            pl.BlockSpec((None, None, NA_BIAS_ROWS, GRID_W, 2 * GRID_W), lambda b, h, i: (layer, h, 0, 0, 0)),
        ],
        out_specs=pl.BlockSpec((NA_TQ, NA_HEAD_DIM), lambda b, h, i: (b * steps + i, h)),
        out_shape=jax.ShapeDtypeStruct((m, NA_WIDTH), BF16),
        compiler_params=pltpu.CompilerParams(
            dimension_semantics=("parallel", "parallel", "parallel"),
            vmem_limit_bytes=VMEM_LIMIT_BYTES,
        ),
        name="natten",
    )(proj, proj, proj, proj, bias)


MIX_TM = 256
GATE_BLOCKS = D_MODEL // LANES


def _mix_kernel(ya_ref, u_ref, v_ref, z_ref, ga_ref, gb_ref, x_ref,
                lng_ref, lnb_ref, ws_ref, bs_ref, wpa_ref, wpb_ref, wout_ref, pg_ref, ng_ref,
                o_ref, hn_ref, vn_ref, yb_ref, m_ref, prev_ref):
    @pl.when(pl.program_id(0) == 0)
    def _():
        prev_ref[...] = jnp.zeros_like(prev_ref)

    x_new = x_ref[...] + _rms_norm_rows(prev_ref[...], pg_ref[...])
    o_ref[...] = x_new
    if hn_ref is not None:
        hn_ref[...] = _rms_norm_rows(x_new, ng_ref[...]).astype(hn_ref.dtype)

    a = jnp.dot(ya_ref[...], wpa_ref[...], preferred_element_type=F32)

    v = [v_ref[g].astype(F32) for g in range(SG_GROUPS)]
    mu = jnp.sum(functools.reduce(jnp.add, v), axis=-1, keepdims=True) * (1.0 / SG_WIDTH)
    xc = [vg - mu for vg in v]
    var = jnp.sum(functools.reduce(jnp.add, [c * c for c in xc]), axis=-1, keepdims=True) * (1.0 / SG_WIDTH)
    inv = lax.rsqrt(var + LN_EPS)
    for g in range(SG_GROUPS):
        vn_ref[g] = (xc[g] * inv * lng_ref[g] + lnb_ref[g]).astype(BF16)
    for c in range(MIX_TM // SG_CHUNK):
        rows = slice(c * SG_CHUNK, (c + 1) * SG_CHUNK)
        for g in range(SG_GROUPS):
            s = jnp.dot(ws_ref[g], vn_ref[g, rows, :], preferred_element_type=F32) + bs_ref[g]
            y = u_ref[g, rows, :].astype(F32) * s * _silu(z_ref[g, rows, :].astype(F32))
            yb_ref[rows, g * SG_GROUP_DIM:(g + 1) * SG_GROUP_DIM] = y.astype(BF16)

    b = jnp.dot(yb_ref[...], wpb_ref[...], preferred_element_type=F32)
    for c in range(GATE_BLOCKS):
        cols = slice(c * LANES, (c + 1) * LANES)
        ga = jax.nn.sigmoid(ga_ref[c].astype(F32))
        gb = jax.nn.sigmoid(gb_ref[c].astype(F32))
        m_ref[:, cols] = (ga * a[:, cols] + gb * b[:, cols]).astype(BF16)
    prev_ref[...] = jnp.dot(m_ref[...], wout_ref[...], preferred_element_type=F32)


def _mix_last_kernel(*refs):
    ins, (o_ref, *scratch) = refs[:16], refs[16:]
    _mix_kernel(*ins, o_ref, None, *scratch)


def _mix(y_a, proj, x2, ln_g, ln_b, w_s, b_s, w_pa, w_pb, w_out, post_g, pre_g, layer, emit_next_h):
    m = x2.shape[0]
    n_tiles = m // MIX_TM
    next_layer = layer + 1 if emit_next_h else layer
    param = lambda shape, l=layer: pl.BlockSpec((None,) + shape, lambda i: (l,) + (0,) * len(shape),
                                                pipeline_mode=pl.Buffered(1))
    cur = lambda i: jnp.minimum(i, n_tiles - 1)
    prev = lambda i: jnp.maximum(i - 1, 0)
    seg = lambda col: pl.BlockSpec((SEG_BLOCKS, MIX_TM, LANES), lambda i: (col, cur(i), 0))
    gate = lambda col: pl.BlockSpec((GATE_BLOCKS, MIX_TM, LANES), lambda i: (col // 2, cur(i), 0))
    tail_tile = pl.BlockSpec((MIX_TM, D_MODEL), lambda i: (prev(i), 0))
    out_shape = [jax.ShapeDtypeStruct((m, D_MODEL), F32)]
    out_specs = [tail_tile]
    if emit_next_h:
        out_shape.append(jax.ShapeDtypeStruct((m, D_MODEL), BF16))
        out_specs.append(tail_tile)
    outs = pl.pallas_call(
        _mix_kernel if emit_next_h else _mix_last_kernel,
        grid=(n_tiles + 1,),
        in_specs=[
            pl.BlockSpec((MIX_TM, NA_WIDTH), lambda i: (cur(i), 0)),
            seg(COL_U), seg(COL_VS), seg(COL_ZB), gate(COL_GA), gate(COL_GB),
            tail_tile,
            param((SG_GROUPS, 1, SG_GROUP_DIM)), param((SG_GROUPS, 1, SG_GROUP_DIM)),
            param((SG_GROUPS, SG_CHUNK, SG_CHUNK)), param((SG_GROUPS, SG_CHUNK, 1)),
            param((NA_WIDTH, D_MODEL)), param((SG_WIDTH, D_MODEL)), param((D_MODEL, D_MODEL)),
            param((1, D_MODEL)), param((1, D_MODEL), next_layer),
        ],
        out_specs=out_specs,
        out_shape=out_shape,
        scratch_shapes=[
            pltpu.VMEM((SG_GROUPS, MIX_TM, SG_GROUP_DIM), BF16),
            pltpu.VMEM((MIX_TM, SG_WIDTH), BF16),
            pltpu.VMEM((MIX_TM, D_MODEL), BF16),
            pltpu.VMEM((MIX_TM, D_MODEL), F32),
        ],
        compiler_params=pltpu.CompilerParams(
            dimension_semantics=("arbitrary",),
            vmem_limit_bytes=VMEM_LIMIT_BYTES,
        ),
        name="mix",
    )(y_a, proj, proj, proj, proj, proj, x2, ln_g, ln_b, w_s, b_s, w_pa, w_pb, w_out, post_g, pre_g)
    return (outs[0], outs[1]) if emit_next_h else (outs[0], None)


def kernel(x, pre_norm_g, post_norm_g, w_in, na_rpb, sg_ln_g, sg_ln_b, sg_w, sg_b, w_proj_a, w_proj_b, w_out):
    b, t, d = x.shape
    depth = w_in.shape[0]
    assert d == D_MODEL and w_in.shape == (depth, D_MODEL, N_IN), (x.shape, w_in.shape)
    assert na_rpb.shape == (depth, NA_HEADS, 2 * NA_MAX_ROWS - 1, 2 * NA_COLS - 1), na_rpb.shape
    assert sg_w.shape == (depth, SG_GROUPS, SG_CHUNK, SG_CHUNK), sg_w.shape
    assert t % NA_TQ == 0 and t // GRID_W >= NA_MAX_ROWS, t
    assert (b * t) % IN_TM == 0 and (b * t) % MIX_TM == 0 and (b * t) % NORM_TM == 0, (b, t)
    x2 = x.reshape(b * t, d)
    pre_g = pre_norm_g.reshape(depth, 1, d)
    post_g = post_norm_g.reshape(depth, 1, d)
    w_pa_b, w_pb_b, w_out_b, w_s_b = (w.astype(BF16) for w in (w_proj_a, w_proj_b, w_out, sg_w))
    bias = _natten_bias_table(na_rpb)
    ln_g = sg_ln_g.reshape(depth, SG_GROUPS, 1, SG_GROUP_DIM)
    ln_b = sg_ln_b.reshape(depth, SG_GROUPS, 1, SG_GROUP_DIM)
    b_s = sg_b.reshape(depth, SG_GROUPS, SG_CHUNK, 1)
    h = _rms_norm(x2, pre_g, 0)
    for l in range(depth):
        proj = _in_proj(h, w_in, l)
        y_a = _natten(proj, bias, l, b)
        x2, h = _mix(y_a, proj, x2, ln_g, ln_b, w_s_b, b_s, w_pa_b, w_pb_b, w_out_b, post_g, pre_g, l,
                     emit_next_h=l + 1 < depth)
    return x2.reshape(b, t, d)
```
